```python
import math
import jax, jax.numpy as jnp
from jax import lax
import numpy as np

D_MODEL = 1024
BATCH = 8
SEQ = 4096
DEPTH = 1

MEM_LEN = 256
EPS = 1e-6

LRU_WIDTH = 512
LRU_BLOCKS = 8
LRU_BLOCK = LRU_WIDTH // LRU_BLOCKS
CONV_WIDTH = 4
LRU_C = 8.0

MLA_HEADS = 8
Q_LORA = 256
KV_LORA = 128
QK_NOPE = 64
QK_ROPE = 32
V_HEAD = 64
MLA_WIDTH = MLA_HEADS * V_HEAD
ROPE_THETA = 10000.0
Q_BLOCK = 128

MEM_HEADS = 4
MEM_HEAD_DIM = 128
MEM_WIDTH = MEM_HEADS * MEM_HEAD_DIM

N_BRANCH = 3
BRANCH_WIDTH = 512

N_GROUPS = 4
EXPERTS_PER_GROUP = 8
N_EXPERTS = N_GROUPS * EXPERTS_PER_GROUP
TOP_K = 2
D_EXPERT = 256

IN_SIZES = (LRU_WIDTH, LRU_WIDTH, Q_LORA, KV_LORA + QK_ROPE, MEM_WIDTH, N_BRANCH * D_MODEL)
D_IN = LRU_WIDTH * 2 + Q_LORA + KV_LORA + QK_ROPE + MEM_WIDTH + N_BRANCH * D_MODEL

kernel_name = "hybrid_rglru_mla_memxattn_hiermoe"


def _split_points(sizes):
    pts, acc = [], 0
    for s in sizes[:-1]:
        acc += s
        pts.append(acc)
    return pts


def rmsnorm(x, g):
    xf = x.astype(jnp.float32)
    y = xf * lax.rsqrt(jnp.mean(xf * xf, axis=-1, keepdims=True) + EPS)
    return (y * g.astype(jnp.float32)).astype(x.dtype)


def rope_angles(positions):
    inv_freq = ROPE_THETA ** (-jnp.arange(0, QK_ROPE, 2, dtype=jnp.float32) / QK_ROPE)
    ang = positions.astype(jnp.float32)[..., None] * inv_freq
    return jnp.cos(ang), jnp.sin(ang)


def apply_rope(t, cos, sin):
    tf = t.astype(jnp.float32)
    half = tf.shape[-1] // 2
    t1, t2 = tf[..., :half], tf[..., half:]
    return jnp.concatenate([t1 * cos - t2 * sin, t2 * cos + t1 * sin], axis=-1).astype(t.dtype)


def causal_depthwise_conv(x, w, b):
    R = x.shape[-1]
    y = lax.conv_general_dilated(
        x, w.astype(x.dtype)[:, None, :], window_strides=(1,), padding=[(CONV_WIDTH - 1, 0)],
        dimension_numbers=('NWC', 'WIO', 'NWC'), feature_group_count=R)
    return y + b.astype(x.dtype)


def rg_lru(xc, w_a, b_a, w_x, b_x, lam):
    B, S, R = xc.shape
    xb = xc.reshape(B, S, LRU_BLOCKS, LRU_BLOCK)
    gate_a = jnp.einsum('bsnc,ncd->bsnd', xb, w_a).reshape(B, S, R) + b_a
    gate_x = jnp.einsum('bsnc,ncd->bsnd', xb, w_x).reshape(B, S, R) + b_x
    r = jax.nn.sigmoid(gate_a.astype(jnp.float32))
    i = jax.nn.sigmoid(gate_x.astype(jnp.float32))
    log_a = -LRU_C * r * jax.nn.softplus(-lam.astype(jnp.float32))
    a = jnp.exp(log_a)
    u = jnp.sqrt(-jnp.expm1(2.0 * log_a)) * (i * xc.astype(jnp.float32))

    def combine(left, right):
        a_l, b_l = left
        a_r, b_r = right
        return a_l * a_r, a_r * b_l + b_r

    _, h = lax.associative_scan(combine, (a, u), axis=1)
    return h.astype(xc.dtype)


def mla_attention(q_nope, q_rope, k_nope, k_rope, v):
    B, S, H, _ = q_nope.shape
    nb = S // Q_BLOCK
    scale = 1.0 / math.sqrt(QK_NOPE + QK_ROPE)

    def to_blocks(t):
        return jnp.moveaxis(t.reshape((B, nb, Q_BLOCK) + t.shape[2:]), 1, 0)

    k_pos = jnp.arange(S)

    def one_block(args):
        qn, qr, start = args
        s = (jnp.einsum('bqhd,bkhd->bhqk', qn, k_nope, preferred_element_type=jnp.float32)
             + jnp.einsum('bqhr,bkr->bhqk', qr, k_rope, preferred_element_type=jnp.float32)) * scale
        q_pos = start + jnp.arange(Q_BLOCK)
        s = jnp.where(k_pos[None, :] <= q_pos[:, None], s, -jnp.inf)
        p = jax.nn.softmax(s, axis=-1).astype(v.dtype)
        return jnp.einsum('bhqk,bkhd->bqhd', p, v)

    starts = jnp.arange(nb) * Q_BLOCK
    out = lax.map(one_block, (to_blocks(q_nope), to_blocks(q_rope), starts))
    return jnp.moveaxis(out, 0, 1).reshape(B, S, H * V_HEAD)


def memory_attention(q, mem_n, w_kv):
    B, S, _ = q.shape
    M = mem_n.shape[1]
    qh = q.reshape(B, S, MEM_HEADS, MEM_HEAD_DIM)
    k, v = jnp.split(mem_n @ w_kv, 2, axis=-1)
    k = k.reshape(B, M, MEM_HEADS, MEM_HEAD_DIM)
    v = v.reshape(B, M, MEM_HEADS, MEM_HEAD_DIM)
    s = jnp.einsum('bshd,bmhd->bhsm', qh, k, preferred_element_type=jnp.float32) / math.sqrt(MEM_HEAD_DIM)
    p = jax.nn.softmax(s, axis=-1).astype(v.dtype)
    return jnp.einsum('bhsm,bmhd->bshd', p, v).reshape(B, S, MEM_WIDTH)


def hier_moe(n2, w_group, b_group, w_router, b_router, w_e_gate, w_e_up, w_e_down):
    B, S, D = n2.shape
    t = n2.reshape(B * S, D)
    g_logits = (t @ w_group).astype(jnp.float32) + b_group.astype(jnp.float32)
    g_prob = jax.nn.softmax(g_logits, axis=-1)
    g_w, g_idx = lax.top_k(g_prob, 1)
    e_logits = ((t @ w_router).astype(jnp.float32) + b_router.astype(jnp.float32)).reshape(-1, N_GROUPS, EXPERTS_PER_GROUP)
    e_in_group = jnp.take_along_axis(e_logits, g_idx[:, :, None], axis=1)[:, 0]
    top_vals, top_idx = lax.top_k(e_in_group, TOP_K)
    top_w = jax.nn.softmax(top_vals, axis=-1) * g_w
    expert_id = g_idx * EXPERTS_PER_GROUP + top_idx
    combine = jnp.einsum('tk,tke->te', top_w, jax.nn.one_hot(expert_id, N_EXPERTS, dtype=jnp.float32)).astype(t.dtype)
    y = jnp.zeros_like(t)
    for e in range(N_EXPERTS):
        hidden = jax.nn.silu(t @ w_e_gate[e]) * (t @ w_e_up[e])
        y = y + combine[:, e:e + 1] * (hidden @ w_e_down[e])
    return y.reshape(B, S, D)


def setup_inputs(seed: int = 0) -> dict:
    key = jax.random.key(seed)
    ks = iter(jax.random.split(key, 40))
    f32 = jnp.float32

    def nrm(shape, fan_in):
        return jax.random.normal(next(ks), shape, f32) * (fan_in ** -0.5)

    def gain(shape):
        return 1.0 + 0.02 * jax.random.normal(next(ks), shape, f32)

    def bias(shape, scale=0.02):
        return scale * jax.random.normal(next(ks), shape, f32)

    x = jax.random.normal(next(ks), (BATCH, SEQ, D_MODEL), f32)
    mem = jax.random.normal(next(ks), (BATCH, MEM_LEN, D_MODEL), f32)
    offsets = jax.random.randint(next(ks), (BATCH, 1), 0, 1024, dtype=jnp.int32)
    positions = offsets + jnp.arange(SEQ, dtype=jnp.int32)[None, :]

    u = jax.random.uniform(next(ks), (DEPTH, LRU_WIDTH), f32, 0.9, 0.999)
    a0 = u ** (1.0 / LRU_C)
    lru_lambda = jnp.log(a0) - jnp.log1p(-a0)

    return {
        "x": x,
        "mem": mem,
        "positions": positions,
        "g_mix": gain((DEPTH, D_MODEL)),
        "w_in": nrm((DEPTH, D_MODEL, D_IN), D_MODEL),
        "conv_w": nrm((DEPTH, CONV_WIDTH, LRU_WIDTH), CONV_WIDTH),
        "conv_b": bias((DEPTH, LRU_WIDTH)),
        "lru_wa": nrm((DEPTH, LRU_BLOCKS, LRU_BLOCK, LRU_BLOCK), LRU_BLOCK),
        "lru_ba": bias((DEPTH, LRU_WIDTH)),
        "lru_wx": nrm((DEPTH, LRU_BLOCKS, LRU_BLOCK, LRU_BLOCK), LRU_BLOCK),
        "lru_bx": bias((DEPTH, LRU_WIDTH)),
        "lru_lambda": lru_lambda,
        "g_q": gain((DEPTH, Q_LORA)),
        "w_uq": nrm((DEPTH, Q_LORA, MLA_HEADS * (QK_NOPE + QK_ROPE)), Q_LORA),
        "g_kv": gain((DEPTH, KV_LORA)),
        "w_ukv": nrm((DEPTH, KV_LORA, MLA_HEADS * (QK_NOPE + V_HEAD)), KV_LORA),
        "g_mem": gain((DEPTH, D_MODEL)),
        "w_mem_kv": nrm((DEPTH, D_MODEL, 2 * MEM_WIDTH), D_MODEL),
        "w_branch": nrm((DEPTH, N_BRANCH, BRANCH_WIDTH, D_MODEL), BRANCH_WIDTH),
        "w_o": nrm((DEPTH, D_MODEL, D_MODEL), D_MODEL),
        "g_ffn": gain((DEPTH, D_MODEL)),
        "w_group": nrm((DEPTH, D_MODEL, N_GROUPS), D_MODEL),
        "b_group": bias((DEPTH, N_GROUPS), 0.01),
        "w_router": nrm((DEPTH, D_MODEL, N_EXPERTS), D_MODEL),
        "b_router": bias((DEPTH, N_EXPERTS), 0.01),
        "w_e_gate": nrm((DEPTH, N_EXPERTS, D_MODEL, D_EXPERT), D_MODEL),
        "w_e_up": nrm((DEPTH, N_EXPERTS, D_MODEL, D_EXPERT), D_MODEL),
        "w_e_down": nrm((DEPTH, N_EXPERTS, D_EXPERT, D_MODEL), D_EXPERT),
        "g_final": gain((D_MODEL,)),
    }


def reference(x, mem, positions, g_mix, w_in, conv_w, conv_b, lru_wa, lru_ba, lru_wx, lru_bx, lru_lambda,
              g_q, w_uq, g_kv, w_ukv, g_mem, w_mem_kv, w_branch, w_o, g_ffn, w_group, b_group,
              w_router, b_router, w_e_gate, w_e_up, w_e_down, g_final):
    B, S, D = x.shape
    cos, sin = rope_angles(positions)
    split_pts = _split_points(IN_SIZES)
    h = x
    for l in range(DEPTH):
        n = rmsnorm(h, g_mix[l])
        proj = n @ w_in[l]
        x_lru, gate_lru, c_q, kv_a, q_mem, gate_logits = jnp.split(proj, split_pts, axis=-1)

        xa = causal_depthwise_conv(x_lru, conv_w[l], conv_b[l])
        y_a = rg_lru(xa, lru_wa[l], lru_ba[l], lru_wx[l], lru_bx[l], lru_lambda[l]) * jax.nn.gelu(gate_lru)

        cq = rmsnorm(c_q, g_q[l])
        q = (cq @ w_uq[l]).reshape(B, S, MLA_HEADS, QK_NOPE + QK_ROPE)
        q_nope = q[..., :QK_NOPE]
        q_rope = apply_rope(q[..., QK_NOPE:], cos[:, :, None, :], sin[:, :, None, :])
        ckv = rmsnorm(kv_a[..., :KV_LORA], g_kv[l])
        k_rope = apply_rope(kv_a[..., KV_LORA:], cos, sin)
        kv = (ckv @ w_ukv[l]).reshape(B, S, MLA_HEADS, QK_NOPE + V_HEAD)
        y_b = mla_attention(q_nope, q_rope, kv[..., :QK_NOPE], k_rope, kv[..., QK_NOPE:])

        mem_n = rmsnorm(mem, g_mem[l])
        y_c = memory_attention(q_mem, mem_n, w_mem_kv[l])

        gl = gate_logits.reshape(B, S, N_BRANCH, D)
        merged = (jax.nn.sigmoid(gl[:, :, 0].astype(jnp.float32)).astype(h.dtype) * (y_a @ w_branch[l, 0])
                  + jax.nn.sigmoid(gl[:, :, 1].astype(jnp.float32)).astype(h.dtype) * (y_b @ w_branch[l, 1])
                  + jax.nn.sigmoid(gl[:, :, 2].astype(jnp.float32)).astype(h.dtype) * (y_c @ w_branch[l, 2]))
        h = h + merged @ w_o[l]

        n2 = rmsnorm(h, g_ffn[l])
        h = h + hier_moe(n2, w_group[l], b_group[l], w_router[l], b_router[l],
                         w_e_gate[l], w_e_up[l], w_e_down[l])
    return rmsnorm(h, g_final)
```

```python
import functools
import math

import jax
import jax.numpy as jnp
from jax import lax
from jax.experimental import pallas as pl
from jax.experimental.pallas import tpu as pltpu

F32 = jnp.float32
BF16 = jnp.bfloat16

EPS = 1e-6
D_MODEL = 1024
LRU_WIDTH = 512
LRU_BLOCKS = 8
CONV_WIDTH = 4
LRU_C = 8.0
MLA_HEADS = 8
Q_LORA = 256
KV_LORA = 128
QK_NOPE = 64
QK_ROPE = 32
V_HEAD = 64
ROPE_THETA = 10000.0
MEM_HEADS = 4
MEM_HEAD_DIM = 128
N_BRANCH = 3
N_GROUPS = 4
EXPERTS_PER_GROUP = 8
N_EXPERTS = N_GROUPS * EXPERTS_PER_GROUP
D_EXPERT = 256

LANES = 128
SUBLANES = 8
HEAD_PAD = LANES
VMEM_LIMIT_BYTES = 56 * 1024 * 1024


def _params(*semantics):
    return pltpu.CompilerParams(dimension_semantics=semantics, vmem_limit_bytes=VMEM_LIMIT_BYTES)


def _rms(x, g):
    return x * lax.rsqrt(jnp.mean(x * x, axis=-1, keepdims=True) + EPS) * g


def _dot(a, b):
    return jnp.dot(a, b, preferred_element_type=F32)


def _dot_nt(a, b):
    return lax.dot_general(a, b, (((1,), (1,)), ((), ())), preferred_element_type=F32)


def _full(shape):
    return pl.BlockSpec(shape, lambda *_: (0,) * len(shape))


def _in_proj_kernel(x_ref, pos_ref, gmix_ref, wa_ref, wcq_ref, wkv_ref, wm_ref, wg_ref, gq_ref, gkv_ref,
                    wq2_ref, wk2_ref, wv_ref, invf_ref,
                    xa_ref, q_ref, k_ref, v_ref, qm_ref, gates_ref):
    nb = _rms(x_ref[...], gmix_ref[...]).astype(BF16)
    xa_ref[...] = _dot(nb, wa_ref[...]).astype(BF16)
    qm_ref[...] = _dot(nb, wm_ref[...]).astype(BF16)
    gates_ref[...] = jax.nn.sigmoid(_dot(nb, wg_ref[...])).astype(BF16)

    lane = lax.broadcasted_iota(jnp.int32, (1, HEAD_PAD), 1)
    rot = (lane >= QK_NOPE) & (lane < QK_NOPE + QK_ROPE)
    ang = pos_ref[...].astype(F32) * invf_ref[...]
    cosv = jnp.where(rot, jnp.cos(ang), 1.0)
    sinv = jnp.where(rot, jnp.sin(ang), 0.0)

    cqn = _rms(_dot(nb, wcq_ref[...]), gq_ref[...]).astype(BF16)
    q2 = _dot(cqn, wq2_ref[...])
    kv3 = _dot(nb, wkv_ref[...])
    ckvn = _rms(kv3[:, :KV_LORA], gkv_ref[...]).astype(BF16)
    k_rope = kv3[:, KV_LORA:KV_LORA + HEAD_PAD] * cosv + kv3[:, KV_LORA + HEAD_PAD:] * sinv
    k2 = _dot(ckvn, wk2_ref[...])
    scale = 1.0 / math.sqrt(QK_NOPE + QK_ROPE)
    width = MLA_HEADS * HEAD_PAD
    for h in range(MLA_HEADS):
        hs = slice(h * HEAD_PAD, (h + 1) * HEAD_PAD)
        hs2 = slice(width + h * HEAD_PAD, width + (h + 1) * HEAD_PAD)
        q_ref[:, hs] = ((q2[:, hs] * cosv + q2[:, hs2] * sinv) * scale).astype(BF16)
        k_ref[:, hs] = (k2[:, hs] + k_rope).astype(BF16)
    v_ref[...] = _dot(ckvn, wv_ref[...]).astype(BF16)


def _in_proj(x2, pos2, g_mix, w_in, g_q, w_uq, g_kv, w_ukv, *, tm):
    t = x2.shape[0]
    d = D_MODEL
    z = lambda *s: jnp.zeros(s, F32)
    half = QK_ROPE // 2
    c0 = 2 * LRU_WIDTH
    c1 = c0 + Q_LORA
    c2 = c1 + KV_LORA
    c3 = c2 + QK_ROPE
    c4 = c3 + MEM_HEADS * MEM_HEAD_DIM
    w_a = w_in[:, :c0]
    w_cq = w_in[:, c0:c1]
    w_ckv = w_in[:, c1:c2]
    t1, t2 = w_in[:, c2:c2 + half], w_in[:, c2 + half:c3]
    w_m = w_in[:, c3:c4]
    w_g = w_in[:, c4:]
    pad = HEAD_PAD - QK_NOPE - QK_ROPE
    kr = jnp.concatenate([z(d, QK_NOPE), t1, t2, z(d, pad)], axis=1)
    krp = jnp.concatenate([z(d, QK_NOPE), -t2, t1, z(d, pad)], axis=1)
    w_kv = jnp.concatenate([w_ckv, kr, krp], axis=1)

    wq = w_uq.reshape(Q_LORA, MLA_HEADS, QK_NOPE + QK_ROPE)
    qn, r1, r2 = wq[..., :QK_NOPE], wq[..., QK_NOPE:QK_NOPE + half], wq[..., QK_NOPE + half:]
    zq = z(Q_LORA, MLA_HEADS, pad)
    wq_plain = jnp.concatenate([qn, r1, r2, zq], axis=-1).reshape(Q_LORA, MLA_HEADS * HEAD_PAD)
    wq_swap = jnp.concatenate([z(Q_LORA, MLA_HEADS, QK_NOPE), -r2, r1, zq], axis=-1).reshape(Q_LORA, MLA_HEADS * HEAD_PAD)
    wq2 = jnp.concatenate([wq_plain, wq_swap], axis=1)

    wkv = w_ukv.reshape(KV_LORA, MLA_HEADS, QK_NOPE + V_HEAD)
    wk2 = jnp.concatenate([wkv[..., :QK_NOPE], z(KV_LORA, MLA_HEADS, HEAD_PAD - QK_NOPE)], axis=-1)
    wk2 = wk2.reshape(KV_LORA, MLA_HEADS * HEAD_PAD)
    wv = wkv[..., QK_NOPE:].reshape(KV_LORA, MLA_HEADS * V_HEAD)

    inv_freq = ROPE_THETA ** (-jnp.arange(0, QK_ROPE, 2, dtype=F32) / QK_ROPE)
    invf = jnp.concatenate([z(QK_NOPE), inv_freq, inv_freq, z(pad)])[None, :]

    weights = [w_a.astype(BF16), w_cq.astype(BF16), w_kv.astype(BF16), w_m.astype(BF16), w_g.astype(BF16)]
    tail = [wq2.astype(BF16), wk2.astype(BF16), wv.astype(BF16), invf]
    args = [x2, pos2, g_mix[None, :]] + weights + [g_q[None, :], g_kv[None, :]] + tail
    row = lambda w: pl.BlockSpec((tm, w), lambda i: (i, 0))
    in_specs = [row(d), row(1)] + [_full(a.shape) for a in args[2:]]
    widths = (2 * LRU_WIDTH, MLA_HEADS * HEAD_PAD, MLA_HEADS * HEAD_PAD, MLA_HEADS * V_HEAD,
              MEM_HEADS * MEM_HEAD_DIM, N_BRANCH * D_MODEL)
    return pl.pallas_call(
        _in_proj_kernel,
        grid=(t // tm,),
        in_specs=in_specs,
        out_specs=[row(w) for w in widths],
        out_shape=[jax.ShapeDtypeStruct((t, w), BF16) for w in widths],
        compiler_params=_params("parallel"),
        name="in_proj",
    )(*args)


def _lru_kernel(xa_ref, cw_ref, cb_ref, wa_ref, ba_ref, wx_ref, bx_ref, lam_ref, ya_ref,
                xbuf, a_s, u_s, hcar, *, ts):
    w = LRU_WIDTH
    halo = SUBLANES

    @pl.when(pl.program_id(1) == 0)
    def _():
        xbuf[0:halo, :] = jnp.zeros((halo, w), F32)
        hcar[...] = jnp.zeros((SUBLANES, w), F32)

    xl = xa_ref[:, :w].astype(F32)
    xbuf[halo:halo + ts, :] = xl
    cw = cw_ref[...]
    xc = cb_ref[...] + cw[CONV_WIDTH - 1:CONV_WIDTH, :] * xl
    for j in range(CONV_WIDTH - 1):
        back = CONV_WIDTH - 1 - j
        xc = xc + cw[j:j + 1, :] * xbuf[halo - back:halo - back + ts, :]
    xbuf[0:halo, :] = xl[ts - halo:ts, :]

    xcb = xc.astype(BF16)
    r = jax.nn.sigmoid(_dot(xcb, wa_ref[...]) + ba_ref[...])
    i = jax.nn.sigmoid(_dot(xcb, wx_ref[...]) + bx_ref[...])
    log_a = -LRU_C * r * jax.nn.softplus(-lam_ref[...])
    a = jnp.exp(log_a)
    u = jnp.sqrt(-jnp.tanh(log_a) * (a * a + 1.0)) * (i * xc)

    rm = lax.broadcasted_iota(jnp.int32, (ts, w), 0) & (SUBLANES - 1)
    for sft in (1, 2, 4):
        keep = rm >= sft
        a_sh = jnp.where(keep, pltpu.roll(a, sft, 0), 1.0)
        u_sh = jnp.where(keep, pltpu.roll(u, sft, 0), 0.0)
        u = u + a * u_sh
        a = a * a_sh
    a_s[...] = a
    u_s[...] = u

    def chain(g, hprev):
        rows = pl.ds(pl.multiple_of(g * SUBLANES, SUBLANES), SUBLANES)
        hh = u_s[rows, :] + a_s[rows, :] * hprev
        u_s[rows, :] = hh
        return jnp.broadcast_to(hh[SUBLANES - 1:SUBLANES, :], (SUBLANES, w))

    hcar[...] = lax.fori_loop(0, ts // SUBLANES, chain, hcar[...])
    gate = xa_ref[:, w:].astype(F32)
    ya_ref[...] = (u_s[...] * jax.nn.gelu(gate)).astype(BF16)


def _block_diag(wb):
    n, c, _ = wb.shape
    eye = jnp.eye(n, dtype=wb.dtype)
    return jnp.einsum('ncd,nm->ncmd', wb, eye).reshape(n * c, n * c)


def _lru(xa, conv_w, conv_b, lru_wa, lru_ba, lru_wx, lru_bx, lam, *, batch, seq, ts):
    w = LRU_WIDTH
    ns = seq // ts
    args = [xa, conv_w, conv_b[None, :], _block_diag(lru_wa).astype(BF16), lru_ba[None, :],
            _block_diag(lru_wx).astype(BF16), lru_bx[None, :], lam[None, :]]
    return pl.pallas_call(
        functools.partial(_lru_kernel, ts=ts),
        grid=(batch, ns),
        in_specs=[pl.BlockSpec((ts, 2 * w), lambda b, s: (b * ns + s, 0))] + [_full(a.shape) for a in args[1:]],
        out_specs=pl.BlockSpec((ts, w), lambda b, s: (b * ns + s, 0)),
        out_shape=jax.ShapeDtypeStruct((batch * seq, w), BF16),
        scratch_shapes=[pltpu.VMEM((ts + SUBLANES, w), F32), pltpu.VMEM((ts, w), F32),
                        pltpu.VMEM((ts, w), F32), pltpu.VMEM((SUBLANES, w), F32)],
        compiler_params=_params("parallel", "arbitrary"),
        name="rg_lru",
    )(*args)


def _attn_kernel(q_ref, k_ref, v_ref, y_ref, *, tq):
    qi = pl.program_id(1)
    row = lax.broadcasted_iota(jnp.int32, (tq, tq), 0)
    col = lax.broadcasted_iota(jnp.int32, (tq, tq), 1)
    causal = col <= row
    lane = lax.broadcasted_iota(jnp.int32, (tq, 2 * V_HEAD), 1)

    def head(h):
        hs = slice(h * HEAD_PAD, (h + 1) * HEAD_PAD)
        vs = slice((h // 2) * 2 * V_HEAD, (h // 2 + 1) * 2 * V_HEAD)
        q = q_ref[:, hs]

        def step(j, carry, masked):
            m, l, acc = carry
            rows = pl.ds(pl.multiple_of(j * tq, tq), tq)
            s = _dot_nt(q, k_ref[rows, hs])
            if masked:
                s = jnp.where(causal, s, -jnp.inf)
            m_new = jnp.maximum(m, jnp.max(s, axis=-1, keepdims=True))
            p = jnp.exp(s - m_new)
            alpha = jnp.exp(m - m_new)
            l = alpha * l + jnp.sum(p, axis=-1, keepdims=True)
            acc = alpha * acc + _dot(p.astype(BF16), v_ref[rows, vs])
            return m_new, l, acc

        init = (jnp.full((tq, 1), -jnp.inf, F32), jnp.zeros((tq, 1), F32), jnp.zeros((tq, 2 * V_HEAD), F32))
        carry = lax.fori_loop(0, qi, functools.partial(step, masked=False), init)
        m, l, acc = step(qi, carry, True)
        return acc / l

    for hp in range(MLA_HEADS // 2):
        even, odd = head(2 * hp), head(2 * hp + 1)
        y_ref[:, hp * 2 * V_HEAD:(hp + 1) * 2 * V_HEAD] = jnp.where(lane < V_HEAD, even, odd).astype(BF16)


def _attention(q, k, v, *, batch, seq, tq):
    nq = seq // tq
    qw = MLA_HEADS * HEAD_PAD
    vw = MLA_HEADS * V_HEAD
    return pl.pallas_call(
        functools.partial(_attn_kernel, tq=tq),
        grid=(batch, nq),
        in_specs=[pl.BlockSpec((tq, qw), lambda b, i: (b * nq + i, 0)),
                  pl.BlockSpec((seq, qw), lambda b, i: (b, 0)),
                  pl.BlockSpec((seq, vw), lambda b, i: (b, 0))],
        out_specs=pl.BlockSpec((tq, vw), lambda b, i: (b * nq + i, 0)),
        out_shape=jax.ShapeDtypeStruct((batch * seq, vw), BF16),
        compiler_params=_params("parallel", "arbitrary"),
        name="latent_attention",
    )(q, k, v)


def _mem_kv_kernel(mem_ref, g_ref, w_ref, kv_ref):
    kv_ref[...] = _dot(_rms(mem_ref[...], g_ref[...]).astype(BF16), w_ref[...]).astype(BF16)


def _mem_kv(mem, g_mem, w_mem_kv):
    b, m, d = mem.shape
    n = w_mem_kv.shape[1]
    return pl.pallas_call(
        _mem_kv_kernel,
        grid=(b,),
        in_specs=[pl.BlockSpec((None, m, d), lambda i: (i, 0, 0)), _full((1, d)), _full((d, n))],
        out_specs=pl.BlockSpec((None, m, n), lambda i: (i, 0, 0)),
        out_shape=jax.ShapeDtypeStruct((b, m, n), BF16),
        compiler_params=_params("parallel"),
        name="mem_kv",
    )(mem, g_mem[None, :], w_mem_kv.astype(BF16))


def _merge_kernel(ya_ref, yb_ref, qm_ref, gates_ref, x_ref, mkv_ref, wb_ref, wo_ref, gffn_ref, wr_ref, br_ref,
                  h_ref, n2_ref, comb_ref):
    mw = MEM_HEADS * MEM_HEAD_DIM
    ycs = []
    for h in range(MEM_HEADS):
        hs = slice(h * MEM_HEAD_DIM, (h + 1) * MEM_HEAD_DIM)
        s = _dot_nt(qm_ref[:, hs], mkv_ref[:, hs]) * (1.0 / math.sqrt(MEM_HEAD_DIM))
        p = jnp.exp(s - jnp.max(s, axis=-1, keepdims=True))
        o = _dot(p.astype(BF16), mkv_ref[:, mw + h * MEM_HEAD_DIM:mw + (h + 1) * MEM_HEAD_DIM])
        ycs.append((o / jnp.sum(p, axis=-1, keepdims=True)).astype(BF16))
    yc = jnp.concatenate(ycs, axis=-1)

    d = D_MODEL
    merged = (gates_ref[:, 0:d].astype(F32) * _dot(ya_ref[...], wb_ref[0])
              + gates_ref[:, d:2 * d].astype(F32) * _dot(yb_ref[...], wb_ref[1])
              + gates_ref[:, 2 * d:3 * d].astype(F32) * _dot(yc, wb_ref[2]))
    hres = x_ref[...] + _dot(merged.astype(BF16), wo_ref[...])
    h_ref[...] = hres
    n2 = _rms(hres, gffn_ref[...])
    n2_ref[...] = n2.astype(BF16)

    lg = jnp.dot(n2, wr_ref[...], preferred_element_type=F32, precision=lax.Precision.HIGHEST) + br_ref[...]
    lane = lax.broadcasted_iota(jnp.int32, lg.shape, 1).astype(F32)
    big = float(4 * LANES)
    neg = -jnp.inf
    gmask = (lane >= N_EXPERTS) & (lane < N_EXPERTS + N_GROUPS)
    gl = jnp.where(gmask, lg, neg)
    gmax = jnp.max(gl, axis=-1, keepdims=True)
    g_w = 1.0 / jnp.sum(jnp.exp(gl - gmax), axis=-1, keepdims=True)
    g_first = jnp.min(jnp.where(gl == gmax, lane, big), axis=-1, keepdims=True) - N_EXPERTS
    lo = g_first * EXPERTS_PER_GROUP
    el = jnp.where((lane >= lo) & (lane < lo + EXPERTS_PER_GROUP), lg, neg)
    v1 = jnp.max(el, axis=-1, keepdims=True)
    i1 = jnp.min(jnp.where(el == v1, lane, big), axis=-1, keepdims=True)
    el2 = jnp.where(lane == i1, neg, el)
    v2 = jnp.max(el2, axis=-1, keepdims=True)
    i2 = jnp.min(jnp.where(el2 == v2, lane, big), axis=-1, keepdims=True)
    e21 = jnp.exp(v2 - v1)
    w1 = 1.0 / (1.0 + e21)
    comb_ref[...] = jnp.where(lane == i1, w1 * g_w, 0.0) + jnp.where(lane == i2, e21 * w1 * g_w, 0.0)


def _merge(ya, yb, qm, gates, x2, mkv, w_branch, w_o, g_ffn, w_group, b_group, w_router, b_router, *, seq, tm):
    t, d = x2.shape
    per_batch = seq // tm
    pad = LANES - N_EXPERTS - N_GROUPS
    wr = jnp.concatenate([w_router, w_group, jnp.zeros((d, pad), F32)], axis=1)
    br = jnp.concatenate([b_router, b_group, jnp.zeros((pad,), F32)])[None, :]
    row = lambda w: pl.BlockSpec((tm, w), lambda i: (i, 0))
    bw = ya.shape[1]
    return pl.pallas_call(
        _merge_kernel,
        grid=(t // tm,),
        in_specs=[row(bw), row(bw), row(bw), row(N_BRANCH * d), row(d),
                  pl.BlockSpec((None,) + mkv.shape[1:], lambda i: (i // per_batch, 0, 0)),
                  _full((N_BRANCH, bw, d)), _full((d, d)), _full((1, d)), _full((d, LANES)), _full((1, LANES))],
        out_specs=[row(d), row(d), row(LANES)],
        out_shape=[jax.ShapeDtypeStruct((t, d), F32), jax.ShapeDtypeStruct((t, d), BF16),
                   jax.ShapeDtypeStruct((t, LANES), F32)],
        compiler_params=_params("parallel"),
        name="merge_route",
    )(ya, yb, qm, gates, x2, mkv, w_branch.astype(BF16), w_o.astype(BF16), g_ffn[None, :], wr, br)


def _moe_kernel(n2_ref, comb_ref, h_ref, wg_ref, wu_ref, wd_ref, gfin_ref, o_ref, acc, *, final_norm):
    e = pl.program_id(1)

    @pl.when(e == 0)
    def _():
        acc[...] = jnp.zeros_like(acc)

    xt = n2_ref[...]
    hidden = jax.nn.silu(_dot(xt, wg_ref[...])) * _dot(xt, wu_ref[...])
    y = _dot(hidden.astype(BF16), wd_ref[...])
    comb = comb_ref[...]
    lane = lax.broadcasted_iota(jnp.int32, comb.shape, 1)
    wcol = jnp.sum(jnp.where(lane == e, comb, 0.0), axis=-1, keepdims=True)
    acc[...] += wcol * y

    @pl.when(e == pl.num_programs(1) - 1)
    def _():
        out = h_ref[...] + acc[...]
        o_ref[...] = _rms(out, gfin_ref[...]) if final_norm else out


def _moe(n2, comb, hres, w_e_gate, w_e_up, w_e_down, g_final, *, tm, final_norm):
    t, d = hres.shape
    ne, _, de = w_e_gate.shape
    return pl.pallas_call(
        functools.partial(_moe_kernel, final_norm=final_norm),
        grid=(t // tm, ne),
        in_specs=[pl.BlockSpec((tm, d), lambda i, e: (i, 0)),
                  pl.BlockSpec((tm, LANES), lambda i, e: (i, 0)),
                  pl.BlockSpec((tm, d), lambda i, e: (i, 0)),
                  pl.BlockSpec((None, d, de), lambda i, e: (e, 0, 0)),
                  pl.BlockSpec((None, d, de), lambda i, e: (e, 0, 0)),
                  pl.BlockSpec((None, de, d), lambda i, e: (e, 0, 0)),
                  _full((1, d))],
        out_specs=pl.BlockSpec((tm, d), lambda i, e: (i, 0)),
        out_shape=jax.ShapeDtypeStruct((t, d), F32),
        scratch_shapes=[pltpu.VMEM((tm, d), F32)],
        compiler_params=_params("parallel", "arbitrary"),
        name="moe",
    )(n2, comb, hres, w_e_gate.astype(BF16), w_e_up.astype(BF16), w_e_down.astype(BF16), g_final[None, :])


def kernel(x, mem, positions, g_mix, w_in, conv_w, conv_b, lru_wa, lru_ba, lru_wx, lru_bx, lru_lambda,
           g_q, w_uq, g_kv, w_ukv, g_mem, w_mem_kv, w_branch, w_o, g_ffn, w_group, b_group,
           w_router, b_router, w_e_gate, w_e_up, w_e_down, g_final):
    batch, seq, d = x.shape
    depth = g_mix.shape[0]
    t = batch * seq
    tile = lambda want, n: math.gcd(want, n)
    h = x.reshape(t, d)
    pos2 = positions.reshape(t, 1)
    for l in range(depth):
        xa, q, k, v, qm, gates = _in_proj(h, pos2, g_mix[l], w_in[l], g_q[l], w_uq[l], g_kv[l], w_ukv[l],
                                          tm=tile(256, seq))
        ya = _lru(xa, conv_w[l], conv_b[l], lru_wa[l], lru_ba[l], lru_wx[l], lru_bx[l], lru_lambda[l],
                  batch=batch, seq=seq, ts=tile(512, seq))
        yb = _attention(q, k, v, batch=batch, seq=seq, tq=tile(256, seq))
        mkv = _mem_kv(mem, g_mem[l], w_mem_kv[l])
        hres, n2, comb = _merge(ya, yb, qm, gates, h, mkv, w_branch[l], w_o[l], g_ffn[l], w_group[l], b_group[l],
                                w_router[l], b_router[l], seq=seq, tm=tile(256, seq))
        h = _moe(n2, comb, hres, w_e_gate[l], w_e_up[l], w_e_down[l], g_final,
                 tm=tile(1024, t), final_norm=(l == depth - 1))
    return h.reshape(batch, seq, d)
```
